```python
import jax, jax.numpy as jnp
from jax import lax
import numpy as np

D_MODEL = 4096
BATCH = 4
SEQ = 2048
DEPTH = 2
DEC_BATCH = 128
DEC_SEQ = 8
PAST_LEN = 16384
PAGE_SIZE = 128

MIX_WIDTH = D_MODEL
POOL_WIDTH = D_MODEL // 4
SCONV_WIDTH = 3 * D_MODEL // 8
CONF_WIDTH = MIX_WIDTH - POOL_WIDTH - SCONV_WIDTH
POOL_WINDOWS = (2, 4, 8, 16)
POOL_GROUPS = len(POOL_WINDOWS)
POOL_GROUP_WIDTH = POOL_WIDTH // POOL_GROUPS
POOL_CTX = max(POOL_WINDOWS) - 1
SCONV_K = 3
CONF_K = 31
D_FF = ((8 * D_MODEL // 3 + 255) // 256) * 256
IN_COLS = POOL_WIDTH + 3 * SCONV_WIDTH + 2 * CONF_WIDTH
EPS = 1e-6

kernel_name = 'hybrid_pool_shortconv_conformer_decoder_step'


def rmsnorm(x, g):
    xf = x.astype(jnp.float32)
    y = xf * lax.rsqrt(jnp.mean(xf * xf, axis=-1, keepdims=True) + EPS)
    return (y * g.astype(jnp.float32)).astype(x.dtype)


def layernorm(x, g, b):
    xf = x.astype(jnp.float32)
    mu = jnp.mean(xf, axis=-1, keepdims=True)
    var = jnp.mean(jnp.square(xf - mu), axis=-1, keepdims=True)
    y = (xf - mu) * lax.rsqrt(var + EPS)
    return (y * g.astype(jnp.float32) + b.astype(jnp.float32)).astype(x.dtype)


def swiglu(x, w_gate, w_up, w_down):
    return (jax.nn.silu(x @ w_gate) * (x @ w_up)) @ w_down


def causal_dwconv(x, ctx, w):
    K, C = w.shape
    xc = jnp.concatenate([ctx.astype(x.dtype), x], axis=1)
    y = lax.conv_general_dilated(xc, w[:, None, :].astype(x.dtype), window_strides=(1,),
                                 padding='VALID', dimension_numbers=('NWC', 'WIO', 'NWC'),
                                 feature_group_count=C)
    return y, xc[:, xc.shape[1] - (K - 1):]


def pool_mixer(v, ctx, pos0, pool_w, pool_scale):
    B, T, _ = v.shape
    vc = jnp.concatenate([ctx.astype(v.dtype), v], axis=1)
    cs = jnp.cumsum(vc.astype(jnp.float32), axis=1)
    cs = jnp.pad(cs, ((0, 0), (1, 0), (0, 0)))
    pos = pos0 + jnp.arange(T)
    end = cs[:, POOL_CTX + 1:POOL_CTX + 1 + T]
    means = []
    for g, w in enumerate(POOL_WINDOWS):
        sl = slice(g * POOL_GROUP_WIDTH, (g + 1) * POOL_GROUP_WIDTH)
        start = cs[:, POOL_CTX + 1 - w:POOL_CTX + 1 - w + T, sl]
        count = jnp.minimum(pos + 1, w).astype(jnp.float32)[None, :, None]
        means.append((end[..., sl] - start) / count)
    mean = jnp.concatenate(means, axis=-1)
    d = (mean - v.astype(jnp.float32)).astype(v.dtype)
    d = d.reshape(B, T, POOL_GROUPS, POOL_GROUP_WIDTH)
    y = jnp.einsum('btgc,gcd->btgd', d, pool_w).reshape(B, T, POOL_WIDTH) * pool_scale
    return y, vc[:, vc.shape[1] - POOL_CTX:]


def mixer_block(xn, ctx_pool, ctx_sconv, ctx_conf, pos0, w_in, pool_w, pool_scale,
                sconv_w, conf_dw_w, conf_dw_b, conf_ln_g, conf_ln_b, w_out):
    proj = xn @ w_in
    o = 0
    v_pool = proj[..., o:o + POOL_WIDTH]; o += POOL_WIDTH
    b_gate = proj[..., o:o + SCONV_WIDTH]; o += SCONV_WIDTH
    c_gate = proj[..., o:o + SCONV_WIDTH]; o += SCONV_WIDTH
    h_sc = proj[..., o:o + SCONV_WIDTH]; o += SCONV_WIDTH
    a_conf = proj[..., o:o + CONF_WIDTH]; o += CONF_WIDTH
    g_conf = proj[..., o:o + CONF_WIDTH]
    y_pool, new_pool = pool_mixer(v_pool, ctx_pool, pos0, pool_w, pool_scale)
    u = c_gate * h_sc
    conv_u, new_sconv = causal_dwconv(u, ctx_sconv, sconv_w)
    y_sc = b_gate * conv_u
    glu = a_conf * jax.nn.sigmoid(g_conf)
    conv_c, new_conf = causal_dwconv(glu, ctx_conf, conf_dw_w)
    y_conf = jax.nn.silu(layernorm(conv_c + conf_dw_b, conf_ln_g, conf_ln_b))
    y = jnp.concatenate([y_pool, y_sc, y_conf], axis=-1) @ w_out
    return y, new_pool, new_sconv, new_conf


def run_layer(x, ctx_pool, ctx_sconv, ctx_conf, pos0, l,
              ffn1_norm, ffn1_w_gate, ffn1_w_up, ffn1_w_down, mix_norm, w_in, pool_w,
              pool_scale, sconv_w, conf_dw_w, conf_dw_b, conf_ln_g, conf_ln_b, w_out,
              ffn2_norm, ffn2_w_gate, ffn2_w_up, ffn2_w_down):
    x = x + 0.5 * swiglu(rmsnorm(x, ffn1_norm[l]), ffn1_w_gate[l], ffn1_w_up[l], ffn1_w_down[l])
    y, new_pool, new_sconv, new_conf = mixer_block(
        rmsnorm(x, mix_norm[l]), ctx_pool, ctx_sconv, ctx_conf, pos0, w_in[l], pool_w[l],
        pool_scale[l], sconv_w[l], conf_dw_w[l], conf_dw_b[l], conf_ln_g[l], conf_ln_b[l], w_out[l])
    x = x + y
    x = x + 0.5 * swiglu(rmsnorm(x, ffn2_norm[l]), ffn2_w_gate[l], ffn2_w_up[l], ffn2_w_down[l])
    return x, new_pool, new_sconv, new_conf


def setup_inputs(seed: int = 0) -> dict:
    key = jax.random.key(seed)
    ks = jax.random.split(key, 24)
    f32 = jnp.float32

    def nrm(k, shape, scale):
        return jax.random.normal(k, shape, f32) * scale

    def gain(k, shape):
        return 1.0 + 0.02 * jax.random.normal(k, shape, f32)

    return {
        'x_prompt': nrm(ks[0], (BATCH, SEQ, D_MODEL), 1.0),
        'x_sample': nrm(ks[1], (DEC_BATCH, DEC_SEQ, D_MODEL), 1.0),
        'state_pool': nrm(ks[2], (DEPTH, DEC_BATCH, POOL_CTX, POOL_WIDTH), 1.0),
        'state_sconv': nrm(ks[3], (DEPTH, DEC_BATCH, SCONV_K - 1, SCONV_WIDTH), 1.0),
        'state_conf': nrm(ks[4], (DEPTH, DEC_BATCH, CONF_K - 1, CONF_WIDTH), 0.5),
        'ffn1_norm': gain(ks[5], (DEPTH, D_MODEL)),
        'ffn1_w_gate': nrm(ks[6], (DEPTH, D_MODEL, D_FF), D_MODEL ** -0.5),
        'ffn1_w_up': nrm(ks[7], (DEPTH, D_MODEL, D_FF), D_MODEL ** -0.5),
        'ffn1_w_down': nrm(ks[8], (DEPTH, D_FF, D_MODEL), D_FF ** -0.5),
        'mix_norm': gain(ks[9], (DEPTH, D_MODEL)),
        'w_in': nrm(ks[10], (DEPTH, D_MODEL, IN_COLS), D_MODEL ** -0.5),
        'pool_w': nrm(ks[11], (DEPTH, POOL_GROUPS, POOL_GROUP_WIDTH, POOL_GROUP_WIDTH), POOL_GROUP_WIDTH ** -0.5),
        'pool_scale': gain(ks[12], (DEPTH, POOL_WIDTH)),
        'sconv_w': nrm(ks[13], (DEPTH, SCONV_K, SCONV_WIDTH), SCONV_K ** -0.5),
        'conf_dw_w': nrm(ks[14], (DEPTH, CONF_K, CONF_WIDTH), CONF_K ** -0.5),
        'conf_dw_b': nrm(ks[15], (DEPTH, CONF_WIDTH), 0.02),
        'conf_ln_g': gain(ks[16], (DEPTH, CONF_WIDTH)),
        'conf_ln_b': nrm(ks[17], (DEPTH, CONF_WIDTH), 0.02),
        'w_out': nrm(ks[18], (DEPTH, MIX_WIDTH, D_MODEL), MIX_WIDTH ** -0.5),
        'ffn2_norm': gain(ks[19], (DEPTH, D_MODEL)),
        'ffn2_w_gate': nrm(ks[20], (DEPTH, D_MODEL, D_FF), D_MODEL ** -0.5),
        'ffn2_w_up': nrm(ks[21], (DEPTH, D_MODEL, D_FF), D_MODEL ** -0.5),
        'ffn2_w_down': nrm(ks[22], (DEPTH, D_FF, D_MODEL), D_FF ** -0.5),
        'final_norm': gain(ks[23], (D_MODEL,)),
    }


def reference(x_prompt, x_sample, state_pool, state_sconv, state_conf,
              ffn1_norm, ffn1_w_gate, ffn1_w_up, ffn1_w_down, mix_norm, w_in, pool_w,
              pool_scale, sconv_w, conf_dw_w, conf_dw_b, conf_ln_g, conf_ln_b, w_out,
              ffn2_norm, ffn2_w_gate, ffn2_w_up, ffn2_w_down, final_norm):
    weights = (ffn1_norm, ffn1_w_gate, ffn1_w_up, ffn1_w_down, mix_norm, w_in, pool_w,
               pool_scale, sconv_w, conf_dw_w, conf_dw_b, conf_ln_g, conf_ln_b, w_out,
               ffn2_norm, ffn2_w_gate, ffn2_w_up, ffn2_w_down)
    xp, xs = x_prompt, x_sample
    bp = xp.shape[0]
    pp_pool, pp_sconv, pp_conf = [], [], []
    ps_pool, ps_sconv, ps_conf = [], [], []
    for l in range(DEPTH):
        z_pool = jnp.zeros((bp, POOL_CTX, POOL_WIDTH), xp.dtype)
        z_sconv = jnp.zeros((bp, SCONV_K - 1, SCONV_WIDTH), xp.dtype)
        z_conf = jnp.zeros((bp, CONF_K - 1, CONF_WIDTH), xp.dtype)
        xp, np_, ns_, nc_ = run_layer(xp, z_pool, z_sconv, z_conf, 0, l, *weights)
        pp_pool.append(np_); pp_sconv.append(ns_); pp_conf.append(nc_)
        xs, np_, ns_, nc_ = run_layer(xs, state_pool[l], state_sconv[l], state_conf[l],
                                      PAST_LEN, l, *weights)
        ps_pool.append(np_); ps_sconv.append(ns_); ps_conf.append(nc_)
    y_prompt = rmsnorm(xp, final_norm)
    y_sample = rmsnorm(xs, final_norm)
    return (y_prompt, y_sample,
            jnp.stack(pp_pool), jnp.stack(pp_sconv), jnp.stack(pp_conf),
            jnp.stack(ps_pool), jnp.stack(ps_sconv), jnp.stack(ps_conf))
```

```python
import functools

import jax
import jax.numpy as jnp
from jax import lax
from jax.experimental import pallas as pl
from jax.experimental.pallas import tpu as pltpu

_F32 = jnp.float32
_BF16 = jnp.bfloat16
_EPS = 1e-6

_POOL_WINDOWS = (2, 4, 8, 16)
_PAST_LEN = 16384

_MIB = 1024 * 1024


def _params(n_axes, vmem_mib):
    return pltpu.CompilerParams(dimension_semantics=("arbitrary",) * n_axes,
                                vmem_limit_bytes=vmem_mib * _MIB)


def _rmsnorm_body(x_ref, g_ref, o_ref):
    x = x_ref[...]
    ms = jnp.mean(x * x, axis=-1, keepdims=True)
    o_ref[...] = ((x * lax.rsqrt(ms + _EPS)) * g_ref[...]).astype(o_ref.dtype)


def _rmsnorm(x, gains, layer, out_dtype, tm=512):
    m, d = x.shape
    g3 = gains.reshape(-1, 1, d)
    return pl.pallas_call(
        _rmsnorm_body,
        grid=(m // tm,),
        in_specs=[pl.BlockSpec((tm, d), lambda i: (i, 0)),
                  pl.BlockSpec((None, 1, d), lambda i: (layer, 0, 0))],
        out_specs=pl.BlockSpec((tm, d), lambda i: (i, 0)),
        out_shape=jax.ShapeDtypeStruct((m, d), out_dtype),
        compiler_params=_params(1, 48),
        name="rmsnorm",
    )(x, g3)


def _gateup_body(x_ref, wg_ref, wu_ref, h_ref):
    x = x_ref[...]
    g = jnp.dot(x, wg_ref[...].astype(_BF16), preferred_element_type=_F32)
    u = jnp.dot(x, wu_ref[...].astype(_BF16), preferred_element_type=_F32)
    h_ref[...] = (jax.nn.silu(g) * u).astype(h_ref.dtype)


def _gateup(xn, w_gate, w_up, layer, tm, tn):
    m, k = xn.shape
    n = w_gate.shape[-1]
    w_spec = pl.BlockSpec((None, k, tn), lambda i, j: (layer, 0, j))
    return pl.pallas_call(
        _gateup_body,
        grid=(m // tm, n // tn),
        in_specs=[pl.BlockSpec((tm, k), lambda i, j: (i, 0)), w_spec, w_spec],
        out_specs=pl.BlockSpec((tm, tn), lambda i, j: (i, j)),
        out_shape=jax.ShapeDtypeStruct((m, n), _BF16),
        compiler_params=_params(2, 56),
        name="ffn_gate_up",
    )(xn, w_gate, w_up)


def _proj_body(x_ref, w_ref, o_ref):
    o_ref[...] = jnp.dot(x_ref[...], w_ref[...].astype(_BF16), preferred_element_type=_F32)


def _proj(xn, w, layer, tm, tn):
    m, k = xn.shape
    n = w.shape[-1]
    return pl.pallas_call(
        _proj_body,
        grid=(m // tm, n // tn),
        in_specs=[pl.BlockSpec((tm, k), lambda i, j: (i, 0)),
                  pl.BlockSpec((None, k, tn), lambda i, j: (layer, 0, j))],
        out_specs=pl.BlockSpec((tm, tn), lambda i, j: (i, j)),
        out_shape=jax.ShapeDtypeStruct((m, n), _F32),
        compiler_params=_params(2, 56),
        name="mixer_in_proj",
    )(xn, w)


def _outproj_body(y_ref, w_ref, x_ref, o_ref):
    o_ref[...] = x_ref[...] + jnp.dot(y_ref[...], w_ref[...].astype(_BF16),
                                      preferred_element_type=_F32)


def _outproj(y, w, x, layer, tm, tn):
    m, k = y.shape
    n = w.shape[-1]
    return pl.pallas_call(
        _outproj_body,
        grid=(m // tm, n // tn),
        in_specs=[pl.BlockSpec((tm, k), lambda i, j: (i, 0)),
                  pl.BlockSpec((None, k, tn), lambda i, j: (layer, 0, j)),
                  pl.BlockSpec((tm, tn), lambda i, j: (i, j))],
        out_specs=pl.BlockSpec((tm, tn), lambda i, j: (i, j)),
        out_shape=jax.ShapeDtypeStruct((m, n), _F32),
        compiler_params=_params(2, 56),
        name="mixer_out_proj",
    )(y, w, x)


def _down_body(h_ref, w_ref, x_ref, o_ref, acc_ref, *, k_tail):
    k = pl.program_id(2)
    last = pl.num_programs(2) - 1

    @pl.when(k == 0)
    def _():
        acc_ref[...] = jnp.dot(h_ref[...], w_ref[...].astype(_BF16), preferred_element_type=_F32)

    @pl.when(jnp.logical_and(k > 0, k < last))
    def _():
        acc_ref[...] += jnp.dot(h_ref[...], w_ref[...].astype(_BF16), preferred_element_type=_F32)

    @pl.when(k == last)
    def _():
        tail = jnp.dot(h_ref[:, :k_tail], w_ref[:k_tail, :].astype(_BF16),
                       preferred_element_type=_F32)
        o_ref[...] = x_ref[...] + 0.5 * (acc_ref[...] + tail)


def _down(h, w, x, layer, tm, tn, tk):
    m, kdim = h.shape
    n = w.shape[-1]
    nk = pl.cdiv(kdim, tk)
    k_tail = kdim - (nk - 1) * tk
    assert nk >= 2
    return pl.pallas_call(
        functools.partial(_down_body, k_tail=k_tail),
        grid=(m // tm, n // tn, nk),
        in_specs=[pl.BlockSpec((tm, tk), lambda i, j, k: (i, k)),
                  pl.BlockSpec((None, tk, tn), lambda i, j, k: (layer, k, j)),
                  pl.BlockSpec((tm, tn), lambda i, j, k: (i, j))],
        out_specs=pl.BlockSpec((tm, tn), lambda i, j, k: (i, j)),
        out_shape=jax.ShapeDtypeStruct((m, n), _F32),
        scratch_shapes=[pltpu.VMEM((tm, tn), _F32)],
        compiler_params=_params(3, 56),
        name="ffn_down",
    )(h, w, x)


def _layernorm_silu(z, lg, lb):
    mu = jnp.mean(z, axis=-1, keepdims=True)
    zc = z - mu
    var = jnp.mean(zc * zc, axis=-1, keepdims=True)
    y = zc * lax.rsqrt(var + _EPS) * lg + lb
    return jax.nn.silu(y)


_POOL_HALO = 16
_SCONV_HALO = 8
_CONF_HALO = 32


def _mixer_prompt_body(proj_ref, pw_ref, ps_ref, sw_ref, cw_ref, cb_ref, lg_ref, lb_ref,
                       y_ref, npool_ref, nsconv_ref, nconf_ref,
                       vbuf, ubuf, gbuf, *, tt, pw, sw, cw):
    t = pl.program_id(1)
    pool_ctx = npool_ref.shape[0]
    sconv_ctx = nsconv_ref.shape[0]
    conf_ctx = nconf_ref.shape[0]
    o_b, o_c, o_h, o_a, o_g = pw, pw + sw, pw + 2 * sw, pw + 3 * sw, pw + 3 * sw + cw

    @pl.when(t == 0)
    def _():
        vbuf[0:_POOL_HALO, :] = jnp.zeros((_POOL_HALO, pw), _F32)
        ubuf[0:_SCONV_HALO, :] = jnp.zeros((_SCONV_HALO, sw), _F32)
        gbuf[0:_CONF_HALO, :] = jnp.zeros((_CONF_HALO, cw), _F32)

    v = proj_ref[:, 0:pw]
    vbuf[_POOL_HALO:_POOL_HALO + tt, :] = v
    pos = t * tt + lax.broadcasted_iota(jnp.int32, (tt, 1), 0)
    gw = pw // len(_POOL_WINDOWS)
    for g, w in enumerate(_POOL_WINDOWS):
        sl = slice(g * gw, (g + 1) * gw)
        s = vbuf[_POOL_HALO:_POOL_HALO + tt, sl]
        for i in range(1, w):
            s = s + vbuf[_POOL_HALO - i:_POOL_HALO - i + tt, sl]
        count = jnp.minimum(pos + 1, w).astype(_F32)
        d = s / count - v[:, sl]
        yg = jnp.dot(d.astype(_BF16), pw_ref[g].astype(_BF16), preferred_element_type=_F32)
        y_ref[:, sl] = (yg * ps_ref[:, sl]).astype(y_ref.dtype)

    u = proj_ref[:, o_c:o_c + sw] * proj_ref[:, o_h:o_h + sw]
    ubuf[_SCONV_HALO:_SCONV_HALO + tt, :] = u
    nk = sw_ref.shape[0]
    conv = None
    for k in range(nk):
        off = _SCONV_HALO - (nk - 1) + k
        term = sw_ref[k:k + 1, :] * ubuf[off:off + tt, :]
        conv = term if conv is None else conv + term
    y_ref[:, pw:pw + sw] = (proj_ref[:, o_b:o_b + sw] * conv).astype(y_ref.dtype)

    glu = proj_ref[:, o_a:o_a + cw] * jax.nn.sigmoid(proj_ref[:, o_g:o_g + cw])
    gbuf[_CONF_HALO:_CONF_HALO + tt, :] = glu
    nck = cw_ref.shape[0]
    acc = None
    for k in range(nck):
        off = _CONF_HALO - (nck - 1) + k
        term = cw_ref[k:k + 1, :] * gbuf[off:off + tt, :]
        acc = term if acc is None else acc + term
    z = acc + cb_ref[...]
    y_ref[:, pw + sw:pw + sw + cw] = _layernorm_silu(z, lg_ref[...], lb_ref[...]).astype(y_ref.dtype)

    vbuf[0:_POOL_HALO, :] = vbuf[tt:tt + _POOL_HALO, :]
    ubuf[0:_SCONV_HALO, :] = ubuf[tt:tt + _SCONV_HALO, :]
    gbuf[0:_CONF_HALO, :] = gbuf[tt:tt + _CONF_HALO, :]

    @pl.when(t == pl.num_programs(1) - 1)
    def _():
        npool_ref[...] = vbuf[_POOL_HALO + tt - pool_ctx:_POOL_HALO + tt, :]
        nsconv_ref[...] = ubuf[_SCONV_HALO + tt - sconv_ctx:_SCONV_HALO + tt, :]
        nconf_ref[...] = gbuf[_CONF_HALO + tt - conf_ctx:_CONF_HALO + tt, :]


def _small_specs(layer, pool_w, pool_scale, sconv_w, conf_dw_w, conf_dw_b, conf_ln_g, conf_ln_b, n_axes):
    def idx(*_):
        return (layer, 0, 0)

    def idx4(*_):
        return (layer, 0, 0, 0)

    row = lambda a: a.reshape(a.shape[0], 1, a.shape[1])
    arrays = [pool_w, row(pool_scale), sconv_w, conf_dw_w, row(conf_dw_b), row(conf_ln_g), row(conf_ln_b)]
    specs = [pl.BlockSpec((None,) + pool_w.shape[1:], idx4)]
    specs += [pl.BlockSpec((None,) + a.shape[1:], idx) for a in arrays[1:]]
    return arrays, specs


def _mixer_prompt(proj, layer, batch, seq, small, ctx_rows, tt=256):
    pool_w = small[0]
    pw = pool_w.shape[1] * pool_w.shape[2]
    sw = small[2].shape[-1]
    cw = small[3].shape[-1]
    d_out = pw + sw + cw
    nt = seq // tt
    arrays, specs = _small_specs(layer, *small, n_axes=2)
    pool_ctx, sconv_ctx, conf_ctx = ctx_rows
    body = functools.partial(_mixer_prompt_body, tt=tt, pw=pw, sw=sw, cw=cw)
    return pl.pallas_call(
        body,
        grid=(batch, nt),
        in_specs=[pl.BlockSpec((tt, proj.shape[1]), lambda b, t: (b * nt + t, 0))] + specs,
        out_specs=[pl.BlockSpec((tt, d_out), lambda b, t: (b * nt + t, 0)),
                   pl.BlockSpec((None, pool_ctx, pw), lambda b, t: (b, 0, 0)),
                   pl.BlockSpec((None, sconv_ctx, sw), lambda b, t: (b, 0, 0)),
                   pl.BlockSpec((None, conf_ctx, cw), lambda b, t: (b, 0, 0))],
        out_shape=[jax.ShapeDtypeStruct((batch * seq, d_out), _BF16),
                   jax.ShapeDtypeStruct((batch, pool_ctx, pw), _F32),
                   jax.ShapeDtypeStruct((batch, sconv_ctx, sw), _F32),
                   jax.ShapeDtypeStruct((batch, conf_ctx, cw), _F32)],
        scratch_shapes=[pltpu.VMEM((_POOL_HALO + tt, pw), _F32),
                        pltpu.VMEM((_SCONV_HALO + tt, sw), _F32),
                        pltpu.VMEM((_CONF_HALO + tt, cw), _F32)],
        compiler_params=_params(2, 56),
        name="mixer_prompt",
    )(proj, *arrays)


def _mixer_sample_body(proj_ref, sp_ref, ss_ref, sc_ref, pw_ref, ps_ref, sw_ref, cw_ref, cb_ref,
                       lg_ref, lb_ref, y_ref, npool_ref, nsconv_ref, nconf_ref,
                       pbuf, sbuf, cbuf, *, ns, dt, pw, sw, cw, pos0):
    pool_ctx = sp_ref.shape[1]
    sconv_ctx = ss_ref.shape[1]
    conf_ctx = sc_ref.shape[1]
    o_b, o_c, o_h, o_a, o_g = pw, pw + sw, pw + 2 * sw, pw + 3 * sw, pw + 3 * sw + cw
    rows = ns * dt

    def seq3(x2):
        return x2.reshape(ns, dt, x2.shape[-1])

    def flat(x3):
        return x3.reshape(rows, x3.shape[-1])

    p0 = pbuf.shape[1] - dt
    v = seq3(proj_ref[:, 0:pw])
    pbuf[:, p0 - pool_ctx:p0, :] = sp_ref[...]
    pbuf[:, p0:p0 + dt, :] = v
    pos = pos0 + lax.broadcasted_iota(jnp.int32, (1, dt, 1), 1)
    gw = pw // len(_POOL_WINDOWS)
    for g, w in enumerate(_POOL_WINDOWS):
        sl = slice(g * gw, (g + 1) * gw)
        s = pbuf[:, p0:p0 + dt, sl]
        for i in range(1, w):
            s = s + pbuf[:, p0 - i:p0 - i + dt, sl]
        count = jnp.minimum(pos + 1, w).astype(_F32)
        d = flat(s / count - v[:, :, sl])
        yg = jnp.dot(d.astype(_BF16), pw_ref[g].astype(_BF16), preferred_element_type=_F32)
        y_ref[:, sl] = (yg * ps_ref[:, sl]).astype(y_ref.dtype)
    npool_ref[...] = pbuf[:, p0 + dt - pool_ctx:p0 + dt, :]

    s0 = sbuf.shape[1] - dt
    u = seq3(proj_ref[:, o_c:o_c + sw] * proj_ref[:, o_h:o_h + sw])
    sbuf[:, s0 - sconv_ctx:s0, :] = ss_ref[...]
    sbuf[:, s0:s0 + dt, :] = u
    nk = sw_ref.shape[0]
    conv = None
    for k in range(nk):
        off = s0 - (nk - 1) + k
        term = sw_ref[k:k + 1, :][None] * sbuf[:, off:off + dt, :]
        conv = term if conv is None else conv + term
    y_ref[:, pw:pw + sw] = (proj_ref[:, o_b:o_b + sw] * flat(conv)).astype(y_ref.dtype)
    nsconv_ref[...] = sbuf[:, s0 + dt - sconv_ctx:s0 + dt, :]

    c0 = cbuf.shape[1] - dt
    glu = seq3(proj_ref[:, o_a:o_a + cw] * jax.nn.sigmoid(proj_ref[:, o_g:o_g + cw]))
    cbuf[:, c0 - conf_ctx:c0, :] = sc_ref[...]
    cbuf[:, c0:c0 + dt, :] = glu
    nck = cw_ref.shape[0]
    acc = None
    for k in range(nck):
        off = c0 - (nck - 1) + k
        term = cw_ref[k:k + 1, :][None] * cbuf[:, off:off + dt, :]
        acc = term if acc is None else acc + term
    z = flat(acc) + cb_ref[...]
    y_ref[:, pw + sw:pw + sw + cw] = _layernorm_silu(z, lg_ref[...], lb_ref[...]).astype(y_ref.dtype)
    nconf_ref[...] = cbuf[:, c0 + dt - conf_ctx:c0 + dt, :]


def _round_up(x, m):
    return (x + m - 1) // m * m


def _mixer_sample(proj, row0, layer, states, small, dec_seq, ns=16):
    state_pool, state_sconv, state_conf = states
    nseq = state_pool.shape[1]
    pool_ctx, pw = state_pool.shape[2:]
    sconv_ctx, sw = state_sconv.shape[2:]
    conf_ctx, cw = state_conf.shape[2:]
    d_out = pw + sw + cw
    rows = ns * dec_seq
    assert row0 % rows == 0 and nseq % ns == 0
    blk0 = row0 // rows
    arrays, specs = _small_specs(layer, *small, n_axes=1)
    body = functools.partial(_mixer_sample_body, ns=ns, dt=dec_seq, pw=pw, sw=sw, cw=cw, pos0=_PAST_LEN)

    def st_in(ctx, width):
        return pl.BlockSpec((None, ns, ctx, width), lambda s: (layer, s, 0, 0))

    def st_out(ctx, width):
        return pl.BlockSpec((ns, ctx, width), lambda s: (s, 0, 0))

    return pl.pallas_call(
        body,
        grid=(nseq // ns,),
        in_specs=[pl.BlockSpec((rows, proj.shape[1]), lambda s: (blk0 + s, 0)),
                  st_in(pool_ctx, pw), st_in(sconv_ctx, sw), st_in(conf_ctx, cw)] + specs,
        out_specs=[pl.BlockSpec((rows, d_out), lambda s: (s, 0)),
                   st_out(pool_ctx, pw), st_out(sconv_ctx, sw), st_out(conf_ctx, cw)],
        out_shape=[jax.ShapeDtypeStruct((nseq * dec_seq, d_out), _BF16),
                   jax.ShapeDtypeStruct((nseq, pool_ctx, pw), _F32),
                   jax.ShapeDtypeStruct((nseq, sconv_ctx, sw), _F32),
                   jax.ShapeDtypeStruct((nseq, conf_ctx, cw), _F32)],
        scratch_shapes=[pltpu.VMEM((ns, _round_up(pool_ctx, 8) + dec_seq, pw), _F32),
                        pltpu.VMEM((ns, _round_up(sconv_ctx, 8) + dec_seq, sw), _F32),
                        pltpu.VMEM((ns, _round_up(conf_ctx, 8) + dec_seq, cw), _F32)],
        compiler_params=_params(1, 56),
        name="mixer_sample",
    )(proj, state_pool, state_sconv, state_conf, *arrays)


def kernel(x_prompt, x_sample, state_pool, state_sconv, state_conf, ffn1_norm, ffn1_w_gate, ffn1_w_up,
           ffn1_w_down, mix_norm, w_in, pool_w, pool_scale, sconv_w, conf_dw_w, conf_dw_b, conf_ln_g,
           conf_ln_b, w_out, ffn2_norm, ffn2_w_gate, ffn2_w_up, ffn2_w_down, final_norm):
    batch, seq, d = x_prompt.shape
    nseq, dec_seq, _ = x_sample.shape
    depth = w_in.shape[0]
    mp = batch * seq
    x = jnp.concatenate([x_prompt.reshape(mp, d), x_sample.reshape(nseq * dec_seq, d)], axis=0)
    ctx_rows = (state_pool.shape[2], state_sconv.shape[2], state_conf.shape[2])
    small = (pool_w, pool_scale, sconv_w, conf_dw_w, conf_dw_b, conf_ln_g, conf_ln_b)
    states = (state_pool, state_sconv, state_conf)

    tm = 1024

    def ffn(x, norm, w_gate, w_up, w_down, layer):
        xn = _rmsnorm(x, norm, layer, _BF16)
        h = _gateup(xn, w_gate, w_up, layer, tm=tm, tn=256)
        return _down(h, w_down, x, layer, tm=tm, tn=1024, tk=1024)

    new_p, new_s = [], []
    for layer in range(depth):
        x = ffn(x, ffn1_norm, ffn1_w_gate, ffn1_w_up, ffn1_w_down, layer)
        xn = _rmsnorm(x, mix_norm, layer, _BF16)
        proj = _proj(xn, w_in, layer, tm=tm, tn=512)
        yp, *st_p = _mixer_prompt(proj, layer, batch, seq, small, ctx_rows)
        ys, *st_s = _mixer_sample(proj, mp, layer, states, small, dec_seq)
        y = jnp.concatenate([yp, ys], axis=0)
        x = _outproj(y, w_out, x, layer, tm=tm, tn=256)
        x = ffn(x, ffn2_norm, ffn2_w_gate, ffn2_w_up, ffn2_w_down, layer)
        new_p.append(st_p)
        new_s.append(st_s)

    y = _rmsnorm(x, final_norm.reshape(1, d), 0, _F32)
    y_prompt = y[:mp].reshape(batch, seq, d)
    y_sample = y[mp:].reshape(nseq, dec_seq, d)
    stack = lambda sts, i: jnp.stack([s[i] for s in sts])
    return (y_prompt, y_sample,
            stack(new_p, 0), stack(new_p, 1), stack(new_p, 2),
            stack(new_s, 0), stack(new_s, 1), stack(new_s, 2))
```

```python
import functools

import jax
import jax.numpy as jnp
from jax import lax
from jax.experimental import pallas as pl
from jax.experimental.pallas import tpu as pltpu

_F32 = jnp.float32
_BF16 = jnp.bfloat16
_EPS = 1e-6

_POOL_WINDOWS = (2, 4, 8, 16)
_PAST_LEN = 16384

_MIB = 1024 * 1024
_V7X_VMEM_BYTES = 64 * _MIB
_V7X_SUBLANES = 8
_V7X_LANES = 128


def _nbytes(shape, dtype):
    n = jnp.dtype(dtype).itemsize
    for s in shape:
        n *= s
    return n


def _params(n_axes, pipelined, resident=()):
    need = 2 * sum(_nbytes(*b) for b in pipelined) + sum(_nbytes(*b) for b in resident)
    limit = min(need + need // 8 + 2 * _MIB, _V7X_VMEM_BYTES - 2 * _MIB)
    return pltpu.CompilerParams(dimension_semantics=("arbitrary",) * n_axes, vmem_limit_bytes=limit)


def _rms(x, g):
    ms = jnp.mean(x * x, axis=-1, keepdims=True)
    return (x * lax.rsqrt(ms + _EPS)) * g


def _rmsnorm_body(x_ref, g_ref, o_ref):
    o_ref[...] = _rms(x_ref[...], g_ref[...]).astype(o_ref.dtype)


def _rmsnorm(x, gains, layer, tm=512):
    m, d = x.shape
    return pl.pallas_call(
        _rmsnorm_body,
        grid=(m // tm,),
        in_specs=[pl.BlockSpec((tm, d), lambda i: (i, 0)),
                  pl.BlockSpec((None, 1, d), lambda i: (layer, 0, 0))],
        out_specs=pl.BlockSpec((tm, d), lambda i: (i, 0)),
        out_shape=jax.ShapeDtypeStruct((m, d), _BF16),
        compiler_params=_params(1, [((tm, d), _F32), ((tm, d), _BF16)], [((tm, d), _F32)] * 2),
        name="rmsnorm",
    )(x, gains.reshape(-1, 1, d))


def _rmsnorm_first_body(xp_ref, xs_ref, g_ref, x_ref, o_ref, *, np_tiles):
    i = pl.program_id(0)

    def emit(src):
        x = src[...]
        x_ref[...] = x
        o_ref[...] = _rms(x, g_ref[...]).astype(o_ref.dtype)

    pl.when(i < np_tiles)(lambda: emit(xp_ref))
    pl.when(i >= np_tiles)(lambda: emit(xs_ref))


def _rmsnorm_first(xp, xs, gains, layer, tm=256):
    (mp, d), ms = xp.shape, xs.shape[0]
    np_tiles, ns_tiles = mp // tm, ms // tm
    assert np_tiles * tm == mp and ns_tiles * tm == ms
    m = mp + ms
    row = lambda i: (i, 0)
    return pl.pallas_call(
        functools.partial(_rmsnorm_first_body, np_tiles=np_tiles),
        grid=(np_tiles + ns_tiles,),
        in_specs=[pl.BlockSpec((tm, d), lambda i: (jnp.minimum(i, np_tiles - 1), 0)),
                  pl.BlockSpec((tm, d), lambda i: (jnp.maximum(i - np_tiles, 0), 0)),
                  pl.BlockSpec((None, 1, d), lambda i: (layer, 0, 0))],
        out_specs=[pl.BlockSpec((tm, d), row), pl.BlockSpec((tm, d), row)],
        out_shape=[jax.ShapeDtypeStruct((m, d), _F32), jax.ShapeDtypeStruct((m, d), _BF16)],
        compiler_params=_params(1, [((tm, d), _F32)] * 3 + [((tm, d), _BF16)], [((tm, d), _F32)] * 2),
        name="rmsnorm_first",
    )(xp, xs, gains.reshape(-1, 1, d))


def _rmsnorm_final_body(x_ref, g_ref, yp_ref, ys_ref, *, np_tiles):
    i = pl.program_id(0)
    y = _rms(x_ref[...], g_ref[...])

    @pl.when(i < np_tiles)
    def _():
        yp_ref[...] = y

    @pl.when(i >= np_tiles)
    def _():
        ys_ref[...] = y


def _rmsnorm_final(x, gain, mp, tm=256):
    m, d = x.shape
    np_tiles = mp // tm
    assert np_tiles * tm == mp and (m - mp) % tm == 0
    return pl.pallas_call(
        functools.partial(_rmsnorm_final_body, np_tiles=np_tiles),
        grid=(m // tm,),
        in_specs=[pl.BlockSpec((tm, d), lambda i: (i, 0)),
                  pl.BlockSpec((1, d), lambda i: (0, 0))],
        out_specs=[pl.BlockSpec((tm, d), lambda i: (jnp.minimum(i, np_tiles - 1), 0)),
                   pl.BlockSpec((tm, d), lambda i: (jnp.maximum(i - np_tiles, 0), 0))],
        out_shape=[jax.ShapeDtypeStruct((mp, d), _F32), jax.ShapeDtypeStruct((m - mp, d), _F32)],
        compiler_params=_params(1, [((tm, d), _F32)] * 3, [((tm, d), _F32)] * 2),
        name="rmsnorm_final",
    )(x, gain.reshape(1, d))


def _gateup_body(x_ref, wg_ref, wu_ref, h_ref):
    x = x_ref[...]
    g = jnp.dot(x, wg_ref[...].astype(_BF16), preferred_element_type=_F32)
    u = jnp.dot(x, wu_ref[...].astype(_BF16), preferred_element_type=_F32)
    h_ref[...] = (jax.nn.silu(g) * u).astype(h_ref.dtype)


def _gateup(xn, w_gate, w_up, layer, tm, tn):
    m, k = xn.shape
    n = w_gate.shape[-1]
    w_spec = pl.BlockSpec((None, k, tn), lambda i, j: (layer, 0, j))
    return pl.pallas_call(
        _gateup_body,
        grid=(m // tm, n // tn),
        in_specs=[pl.BlockSpec((tm, k), lambda i, j: (i, 0)), w_spec, w_spec],
        out_specs=pl.BlockSpec((tm, tn), lambda i, j: (i, j)),
        out_shape=jax.ShapeDtypeStruct((m, n), _BF16),
        compiler_params=_params(2, [((tm, k), _BF16), ((k, tn), _F32), ((k, tn), _F32), ((tm, tn), _BF16)],
                                [((k, tn), _BF16)] * 2 + [((tm, tn), _F32)] * 3),
        name="ffn_gate_up",
    )(xn, w_gate, w_up)


def _proj_body(x_ref, w_ref, o_ref):
    o_ref[...] = jnp.dot(x_ref[...], w_ref[...].astype(_BF16), preferred_element_type=_F32)


def _proj(xn, w, layer, tm, tn):
    m, k = xn.shape
    n = w.shape[-1]
    return pl.pallas_call(
        _proj_body,
        grid=(m // tm, n // tn),
        in_specs=[pl.BlockSpec((tm, k), lambda i, j: (i, 0)),
                  pl.BlockSpec((None, k, tn), lambda i, j: (layer, 0, j))],
        out_specs=pl.BlockSpec((tm, tn), lambda i, j: (i, j)),
        out_shape=jax.ShapeDtypeStruct((m, n), _F32),
        compiler_params=_params(2, [((tm, k), _BF16), ((k, tn), _F32), ((tm, tn), _F32)],
                                [((k, tn), _BF16), ((tm, tn), _F32)]),
        name="mixer_in_proj",
    )(xn, w)


def _outproj_body(yp_ref, ys_ref, w_ref, x_ref, o_ref, *, np_tiles):
    i = pl.program_id(0)

    def emit(y_ref):
        o_ref[...] = x_ref[...] + jnp.dot(y_ref[...], w_ref[...].astype(_BF16),
                                          preferred_element_type=_F32)

    pl.when(i < np_tiles)(lambda: emit(yp_ref))
    pl.when(i >= np_tiles)(lambda: emit(ys_ref))


def _outproj(yp, ys, w, x, layer, tm, tn):
    (mp, k), ms = yp.shape, ys.shape[0]
    n = w.shape[-1]
    np_tiles, ns_tiles = mp // tm, ms // tm
    assert np_tiles * tm == mp and ns_tiles * tm == ms
    return pl.pallas_call(
        functools.partial(_outproj_body, np_tiles=np_tiles),
        grid=(np_tiles + ns_tiles, n // tn),
        in_specs=[pl.BlockSpec((tm, k), lambda i, j: (jnp.minimum(i, np_tiles - 1), 0)),
                  pl.BlockSpec((tm, k), lambda i, j: (jnp.maximum(i - np_tiles, 0), 0)),
                  pl.BlockSpec((None, k, tn), lambda i, j: (layer, 0, j)),
                  pl.BlockSpec((tm, tn), lambda i, j: (i, j))],
        out_specs=pl.BlockSpec((tm, tn), lambda i, j: (i, j)),
        out_shape=jax.ShapeDtypeStruct((mp + ms, n), _F32),
        compiler_params=_params(2, [((tm, k), _BF16)] * 2 + [((k, tn), _F32)] + [((tm, tn), _F32)] * 2,
                                [((k, tn), _BF16), ((tm, tn), _F32)]),
        name="mixer_out_proj",
    )(yp, ys, w, x)


_DOWN_X_CHUNKS = 8


def _down_body(h_ref, w_ref, x_ref, o_ref, *, k_tail, x_rows):
    k = pl.program_id(2)
    last = pl.num_programs(2) - 1

    def half_dot(hs, ws):
        return 0.5 * jnp.dot(hs, ws.astype(_BF16), preferred_element_type=_F32)

    @pl.when(k == 0)
    def _():
        o_ref[...] = half_dot(h_ref[...], w_ref[...])

    @pl.when(jnp.logical_and(k > 0, k < last))
    def _():
        o_ref[...] += half_dot(h_ref[...], w_ref[...])

    @pl.when(k == last)
    def _():
        o_ref[...] += half_dot(h_ref[:, :k_tail], w_ref[:k_tail, :])

    @pl.when(k < _DOWN_X_CHUNKS)
    def _():
        r0 = pl.multiple_of(k * x_rows, x_rows)
        o_ref[pl.ds(r0, x_rows), :] += x_ref[...]


def _down(h, w, x, layer, tm, tn, tk):
    m, kdim = h.shape
    n = w.shape[-1]
    nk = pl.cdiv(kdim, tk)
    k_tail = kdim - (nk - 1) * tk
    x_rows = tm // _DOWN_X_CHUNKS
    assert nk > _DOWN_X_CHUNKS and x_rows * _DOWN_X_CHUNKS == tm and x_rows % _V7X_SUBLANES == 0
    return pl.pallas_call(
        functools.partial(_down_body, k_tail=k_tail, x_rows=x_rows),
        grid=(m // tm, n // tn, nk),
        in_specs=[pl.BlockSpec((tm, tk), lambda i, j, k: (i, k)),
                  pl.BlockSpec((None, tk, tn), lambda i, j, k: (layer, k, j)),
                  pl.BlockSpec((x_rows, tn),
                               lambda i, j, k: (i * _DOWN_X_CHUNKS + jnp.minimum(k, _DOWN_X_CHUNKS - 1), j))],
        out_specs=pl.BlockSpec((tm, tn), lambda i, j, k: (i, j)),
        out_shape=jax.ShapeDtypeStruct((m, n), _F32),
        compiler_params=_params(3, [((tm, tk), _BF16), ((tk, tn), _F32), ((x_rows, tn), _F32), ((tm, tn), _F32)],
                                [((tk, tn), _BF16)]),
        name="ffn_down",
    )(h, w, x)


def _layernorm_silu(z, lg, lb):
    mu = jnp.mean(z, axis=-1, keepdims=True)
    zc = z - mu
    var = jnp.mean(zc * zc, axis=-1, keepdims=True)
    y = zc * lax.rsqrt(var + _EPS) * lg + lb
    return jax.nn.silu(y)


_POOL_HALO = 16
_SCONV_HALO = 8
_CONF_HALO = 32
_CONV_ROWS = 64
_CONV_LANES = _V7X_LANES


def _dwconv_rows(buf, w_ref, z_ref, bias_ref, *, halo, tt, width):
    nk = w_ref.shape[0]
    base = halo - (nk - 1)
    span = _V7X_SUBLANES * ((base + nk - 1) // _V7X_SUBLANES)
    win_rows = _CONV_ROWS + span
    assert base >= 0 and tt % _CONV_ROWS == 0 and width % _CONV_LANES == 0

    def chunk(ci, carry):
        r0 = pl.multiple_of(ci * _CONV_ROWS, _CONV_ROWS)
        for c0 in range(0, width, _CONV_LANES):
            cols = slice(c0, c0 + _CONV_LANES)
            win = buf[pl.ds(r0, win_rows), cols]
            acc = jnp.broadcast_to(bias_ref[:, cols], (_CONV_ROWS, _CONV_LANES))
            for r in range(_V7X_SUBLANES):
                taps = [k for k in range(nk) if (base + k) % _V7X_SUBLANES == r]
                if not taps:
                    continue
                assert r == 0 or max(base + k for k in taps) + _CONV_ROWS <= win_rows
                shifted = win if r == 0 else pltpu.roll(win, win_rows - r, 0)
                for k in taps:
                    q0 = _V7X_SUBLANES * ((base + k) // _V7X_SUBLANES)
                    acc = acc + w_ref[k:k + 1, cols] * shifted[q0:q0 + _CONV_ROWS, :]
            z_ref[pl.ds(r0, _CONV_ROWS), cols] = acc
        return carry

    lax.fori_loop(0, tt // _CONV_ROWS, chunk, 0)


def _mixer_prompt_body(proj_ref, pw_ref, ps_ref, sw_ref, cw_ref, cb_ref, lg_ref, lb_ref,
                       y_ref, npool_ref, nsconv_ref, nconf_ref,
                       vbuf, ubuf, gbuf, zbuf, *, tt, pw, sw, cw):
    t = pl.program_id(1)
    pool_ctx = npool_ref.shape[0]
    sconv_ctx = nsconv_ref.shape[0]
    conf_ctx = nconf_ref.shape[0]
    o_b, o_c, o_h, o_a, o_g = pw, pw + sw, pw + 2 * sw, pw + 3 * sw, pw + 3 * sw + cw

    @pl.when(t == 0)
    def _():
        vbuf[0:_POOL_HALO, :] = jnp.zeros((_POOL_HALO, pw), _F32)
        ubuf[0:_SCONV_HALO, :] = jnp.zeros((_SCONV_HALO, sw), _F32)
        gbuf[0:_CONF_HALO, :] = jnp.zeros((_CONF_HALO, cw), _F32)

    v = proj_ref[:, 0:pw]
    vbuf[_POOL_HALO:_POOL_HALO + tt, :] = v
    pos = t * tt + lax.broadcasted_iota(jnp.int32, (tt, 1), 0)
    gw = pw // len(_POOL_WINDOWS)
    for g, w in enumerate(_POOL_WINDOWS):
        sl = slice(g * gw, (g + 1) * gw)
        s = vbuf[_POOL_HALO:_POOL_HALO + tt, sl]
        for i in range(1, w):
            s = s + vbuf[_POOL_HALO - i:_POOL_HALO - i + tt, sl]
        count = jnp.minimum(pos + 1, w).astype(_F32)
        d = s / count - v[:, sl]
        yg = jnp.dot(d.astype(_BF16), pw_ref[g].astype(_BF16), preferred_element_type=_F32)
        y_ref[:, sl] = (yg * ps_ref[:, sl]).astype(y_ref.dtype)

    u = proj_ref[:, o_c:o_c + sw] * proj_ref[:, o_h:o_h + sw]
    ubuf[_SCONV_HALO:_SCONV_HALO + tt, :] = u
    nk = sw_ref.shape[0]
    conv = None
    for k in range(nk):
        off = _SCONV_HALO - (nk - 1) + k
        term = sw_ref[k:k + 1, :] * ubuf[off:off + tt, :]
        conv = term if conv is None else conv + term
    y_ref[:, pw:pw + sw] = (proj_ref[:, o_b:o_b + sw] * conv).astype(y_ref.dtype)

    glu = proj_ref[:, o_a:o_a + cw] * jax.nn.sigmoid(proj_ref[:, o_g:o_g + cw])
    gbuf[_CONF_HALO:_CONF_HALO + tt, :] = glu
    _dwconv_rows(gbuf, cw_ref, zbuf, cb_ref, halo=_CONF_HALO, tt=tt, width=cw)
    y_ref[:, pw + sw:pw + sw + cw] = _layernorm_silu(zbuf[...], lg_ref[...], lb_ref[...]).astype(y_ref.dtype)

    vbuf[0:_POOL_HALO, :] = vbuf[tt:tt + _POOL_HALO, :]
    ubuf[0:_SCONV_HALO, :] = ubuf[tt:tt + _SCONV_HALO, :]
    gbuf[0:_CONF_HALO, :] = gbuf[tt:tt + _CONF_HALO, :]

    @pl.when(t == pl.num_programs(1) - 1)
    def _():
        npool_ref[...] = vbuf[_POOL_HALO + tt - pool_ctx:_POOL_HALO + tt, :]
        nsconv_ref[...] = ubuf[_SCONV_HALO + tt - sconv_ctx:_SCONV_HALO + tt, :]
        nconf_ref[...] = gbuf[_CONF_HALO + tt - conf_ctx:_CONF_HALO + tt, :]


def _small_specs(layer, pool_w, pool_scale, sconv_w, conf_dw_w, conf_dw_b, conf_ln_g, conf_ln_b):
    def idx(*_):
        return (layer, 0, 0)

    def idx4(*_):
        return (layer, 0, 0, 0)

    row = lambda a: a.reshape(a.shape[0], 1, a.shape[1])
    arrays = [pool_w, row(pool_scale), sconv_w, conf_dw_w, row(conf_dw_b), row(conf_ln_g), row(conf_ln_b)]
    specs = [pl.BlockSpec((None,) + pool_w.shape[1:], idx4)]
    specs += [pl.BlockSpec((None,) + a.shape[1:], idx) for a in arrays[1:]]
    blocks = [(a.shape[1:], _F32) for a in arrays]
    return arrays, specs, blocks


def _mixer_prompt(proj, layer, batch, seq, small, ctx_rows, tt=256):
    pool_w = small[0]
    pw = pool_w.shape[1] * pool_w.shape[2]
    sw = small[2].shape[-1]
    cw = small[3].shape[-1]
    d_in = proj.shape[1]
    d_out = pw + sw + cw
    nt = seq // tt
    arrays, specs, small_blocks = _small_specs(layer, *small)
    pool_ctx, sconv_ctx, conf_ctx = ctx_rows
    body = functools.partial(_mixer_prompt_body, tt=tt, pw=pw, sw=sw, cw=cw)
    scratch = [((_POOL_HALO + tt, pw), _F32), ((_SCONV_HALO + tt, sw), _F32),
               ((_CONF_HALO + tt, cw), _F32), ((tt, cw), _F32)]
    state_blocks = [((pool_ctx, pw), _F32), ((sconv_ctx, sw), _F32), ((conf_ctx, cw), _F32)]
    return pl.pallas_call(
        body,
        grid=(batch, nt),
        in_specs=[pl.BlockSpec((tt, d_in), lambda b, t: (b * nt + t, 0))] + specs,
        out_specs=[pl.BlockSpec((tt, d_out), lambda b, t: (b * nt + t, 0)),
                   pl.BlockSpec((None, pool_ctx, pw), lambda b, t: (b, 0, 0)),
                   pl.BlockSpec((None, sconv_ctx, sw), lambda b, t: (b, 0, 0)),
                   pl.BlockSpec((None, conf_ctx, cw), lambda b, t: (b, 0, 0))],
        out_shape=[jax.ShapeDtypeStruct((batch * seq, d_out), _BF16),
                   jax.ShapeDtypeStruct((batch, pool_ctx, pw), _F32),
                   jax.ShapeDtypeStruct((batch, sconv_ctx, sw), _F32),
                   jax.ShapeDtypeStruct((batch, conf_ctx, cw), _F32)],
        scratch_shapes=[pltpu.VMEM(s, d) for s, d in scratch],
        compiler_params=_params(2, [((tt, d_in), _F32), ((tt, d_out), _BF16)] + small_blocks + state_blocks,
                                scratch + [((tt, cw), _F32)] * 6),
        name="mixer_prompt",
    )(proj, *arrays)


def _mixer_sample_body(proj_ref, sp_ref, ss_ref, sc_ref, pw_ref, ps_ref, sw_ref, cw_ref, cb_ref,
                       lg_ref, lb_ref, y_ref, npool_ref, nsconv_ref, nconf_ref,
                       pbuf, sbuf, cbuf, *, ns, dt, pw, sw, cw, pos0):
    pool_ctx = sp_ref.shape[1]
    sconv_ctx = ss_ref.shape[1]
    conf_ctx = sc_ref.shape[1]
    o_b, o_c, o_h, o_a, o_g = pw, pw + sw, pw + 2 * sw, pw + 3 * sw, pw + 3 * sw + cw
    rows = ns * dt

    def seq3(x2):
        return x2.reshape(ns, dt, x2.shape[-1])

    def flat(x3):
        return x3.reshape(rows, x3.shape[-1])

    p0 = pbuf.shape[1] - dt
    v = seq3(proj_ref[:, 0:pw])
    pbuf[:, p0 - pool_ctx:p0, :] = sp_ref[...]
    pbuf[:, p0:p0 + dt, :] = v
    pos = pos0 + lax.broadcasted_iota(jnp.int32, (1, dt, 1), 1)
    gw = pw // len(_POOL_WINDOWS)
    for g, w in enumerate(_POOL_WINDOWS):
        sl = slice(g * gw, (g + 1) * gw)
        s = pbuf[:, p0:p0 + dt, sl]
        for i in range(1, w):
            s = s + pbuf[:, p0 - i:p0 - i + dt, sl]
        count = jnp.minimum(pos + 1, w).astype(_F32)
        d = flat(s / count - v[:, :, sl])
        yg = jnp.dot(d.astype(_BF16), pw_ref[g].astype(_BF16), preferred_element_type=_F32)
        y_ref[:, sl] = (yg * ps_ref[:, sl]).astype(y_ref.dtype)
    npool_ref[...] = pbuf[:, p0 + dt - pool_ctx:p0 + dt, :]

    s0 = sbuf.shape[1] - dt
    u = seq3(proj_ref[:, o_c:o_c + sw] * proj_ref[:, o_h:o_h + sw])
    sbuf[:, s0 - sconv_ctx:s0, :] = ss_ref[...]
    sbuf[:, s0:s0 + dt, :] = u
    nk = sw_ref.shape[0]
    conv = None
    for k in range(nk):
        off = s0 - (nk - 1) + k
        term = sw_ref[k:k + 1, :][None] * sbuf[:, off:off + dt, :]
        conv = term if conv is None else conv + term
    y_ref[:, pw:pw + sw] = (proj_ref[:, o_b:o_b + sw] * flat(conv)).astype(y_ref.dtype)
    nsconv_ref[...] = sbuf[:, s0 + dt - sconv_ctx:s0 + dt, :]

    c0 = cbuf.shape[1] - dt
    glu = seq3(proj_ref[:, o_a:o_a + cw] * jax.nn.sigmoid(proj_ref[:, o_g:o_g + cw]))
    cbuf[:, c0 - conf_ctx:c0, :] = sc_ref[...]
    cbuf[:, c0:c0 + dt, :] = glu
    nck = cw_ref.shape[0]
    acc = None
    for k in range(nck):
        off = c0 - (nck - 1) + k
        term = cw_ref[k:k + 1, :][None] * cbuf[:, off:off + dt, :]
        acc = term if acc is None else acc + term
    z = flat(acc) + cb_ref[...]
    y_ref[:, pw + sw:pw + sw + cw] = _layernorm_silu(z, lg_ref[...], lb_ref[...]).astype(y_ref.dtype)
    nconf_ref[...] = cbuf[:, c0 + dt - conf_ctx:c0 + dt, :]


def _round_up(x, m):
    return (x + m - 1) // m * m


def _mixer_sample(proj, row0, layer, states, small, dec_seq, ns=16):
    state_pool, state_sconv, state_conf = states
    nseq = state_pool.shape[1]
    pool_ctx, pw = state_pool.shape[2:]
    sconv_ctx, sw = state_sconv.shape[2:]
    conf_ctx, cw = state_conf.shape[2:]
    d_in = proj.shape[1]
    d_out = pw + sw + cw
    rows = ns * dec_seq
    assert row0 % rows == 0 and nseq % ns == 0
    blk0 = row0 // rows
    arrays, specs, small_blocks = _small_specs(layer, *small)
    body = functools.partial(_mixer_sample_body, ns=ns, dt=dec_seq, pw=pw, sw=sw, cw=cw, pos0=_PAST_LEN)

    def st_in(ctx, width):
        return pl.BlockSpec((None, ns, ctx, width), lambda s: (layer, s, 0, 0))

    def st_out(ctx, width):
        return pl.BlockSpec((ns, ctx, width), lambda s: (s, 0, 0))

    pad = lambda ctx: _round_up(ctx, _V7X_SUBLANES)
    scratch = [((ns, pad(pool_ctx) + dec_seq, pw), _F32), ((ns, pad(sconv_ctx) + dec_seq, sw), _F32),
               ((ns, pad(conf_ctx) + dec_seq, cw), _F32)]
    state_blocks = [((ns, pad(pool_ctx), pw), _F32), ((ns, pad(sconv_ctx), sw), _F32),
                    ((ns, pad(conf_ctx), cw), _F32)]
    return pl.pallas_call(
        body,
        grid=(nseq // ns,),
        in_specs=[pl.BlockSpec((rows, d_in), lambda s: (blk0 + s, 0)),
                  st_in(pool_ctx, pw), st_in(sconv_ctx, sw), st_in(conf_ctx, cw)] + specs,
        out_specs=[pl.BlockSpec((rows, d_out), lambda s: (s, 0)),
                   st_out(pool_ctx, pw), st_out(sconv_ctx, sw), st_out(conf_ctx, cw)],
        out_shape=[jax.ShapeDtypeStruct((nseq * dec_seq, d_out), _BF16),
                   jax.ShapeDtypeStruct((nseq, pool_ctx, pw), _F32),
                   jax.ShapeDtypeStruct((nseq, sconv_ctx, sw), _F32),
                   jax.ShapeDtypeStruct((nseq, conf_ctx, cw), _F32)],
        scratch_shapes=[pltpu.VMEM(s, d) for s, d in scratch],
        compiler_params=_params(1, [((rows, d_in), _F32), ((rows, d_out), _BF16)] + small_blocks + state_blocks * 2,
                                scratch + [((rows, cw), _F32)] * 6),
        name="mixer_sample",
    )(proj, state_pool, state_sconv, state_conf, *arrays)


def kernel(x_prompt, x_sample, state_pool, state_sconv, state_conf, ffn1_norm, ffn1_w_gate, ffn1_w_up,
           ffn1_w_down, mix_norm, w_in, pool_w, pool_scale, sconv_w, conf_dw_w, conf_dw_b, conf_ln_g,
           conf_ln_b, w_out, ffn2_norm, ffn2_w_gate, ffn2_w_up, ffn2_w_down, final_norm):
    batch, seq, d = x_prompt.shape
    nseq, dec_seq, _ = x_sample.shape
    depth = w_in.shape[0]
    mp, ms = batch * seq, nseq * dec_seq
    ctx_rows = (state_pool.shape[2], state_sconv.shape[2], state_conf.shape[2])
    small = (pool_w, pool_scale, sconv_w, conf_dw_w, conf_dw_b, conf_ln_g, conf_ln_b)
    states = (state_pool, state_sconv, state_conf)

    def ffn(x, xn, w_gate, w_up, w_down, layer):
        h = _gateup(xn, w_gate, w_up, layer, tm=1536, tn=256)
        return _down(h, w_down, x, layer, tm=4608, tn=1024, tk=512)

    new_p, new_s = [], []
    x = None
    for layer in range(depth):
        if layer == 0:
            x, xn = _rmsnorm_first(x_prompt.reshape(mp, d), x_sample.reshape(ms, d), ffn1_norm, layer)
        else:
            xn = _rmsnorm(x, ffn1_norm, layer)
        x = ffn(x, xn, ffn1_w_gate, ffn1_w_up, ffn1_w_down, layer)
        proj = _proj(_rmsnorm(x, mix_norm, layer), w_in, layer, tm=1536, tn=512)
        yp, *st_p = _mixer_prompt(proj, layer, batch, seq, small, ctx_rows)
        ys, *st_s = _mixer_sample(proj, mp, layer, states, small, dec_seq)
        x = _outproj(yp, ys, w_out, x, layer, tm=1024, tn=256)
        x = ffn(x, _rmsnorm(x, ffn2_norm, layer), ffn2_w_gate, ffn2_w_up, ffn2_w_down, layer)
        new_p.append(st_p)
        new_s.append(st_s)

    y_prompt, y_sample = _rmsnorm_final(x, final_norm, mp)
    stack = lambda sts, i: jnp.stack([s[i] for s in sts])
    return (y_prompt.reshape(batch, seq, d), y_sample.reshape(nseq, dec_seq, d),
            stack(new_p, 0), stack(new_p, 1), stack(new_p, 2),
            stack(new_s, 0), stack(new_s, 1), stack(new_s, 2))
```
